```python
import jax, jax.numpy as jnp
from jax import lax
import numpy as np

D_MODEL = 1024
BATCH = 8
SEQ = 2048
DEPTH = 4
DEC_BATCH = 128
DEC_SEQ = 8
PAST_LEN = 2048
PAGE_SIZE = 128

N_A_LAYERS = DEPTH // 2
N_B_LAYERS = DEPTH - N_A_LAYERS
A_HEADS = 8
A_DV = D_MODEL // A_HEADS
A_DK = A_DV // 2
A_HK = A_HEADS * A_DK
A_HV = A_HEADS * A_DV
A_PROJ = 2 * A_HK + 2 * A_HV + 2 * A_HEADS
MLSTM_CHUNK = 64
B_HEADS = 8
B_HD = D_MODEL // B_HEADS
B_HW = B_HEADS * B_HD
MOBA_BLOCK = 256
MOBA_TOPK = 3
PROMPT_Q_BLOCK = 16
D_FF = 2816
EPS = 1e-6

kernel_name = "yoco_mlstm_moba_macaron_step"

F32 = jnp.float32


def rmsnorm(x, g):
    xf = x.astype(F32)
    y = xf * lax.rsqrt(jnp.mean(xf * xf, axis=-1, keepdims=True) + EPS)
    return (y * g.astype(F32)).astype(x.dtype)


def swiglu(h, w_in, w_out):
    gu = h @ w_in
    g, u = jnp.split(gu, 2, axis=-1)
    return (jax.nn.silu(g) * u) @ w_out


def alibi_slopes():
    return 2.0 ** (-8.0 * jnp.arange(1, B_HEADS + 1, dtype=F32) / B_HEADS)


def mlstm_cell(q, k, v, logi, logf, c0, n0, m0, chunk):
    bt, t, nh, _ = q.shape
    dv = v.shape[-1]
    nc = t // chunk

    def to_chunks(a):
        return a.astype(F32).reshape(bt, nc, chunk, nh, -1).transpose(1, 0, 3, 2, 4)

    def gate_chunks(a):
        return a.astype(F32).reshape(bt, nc, chunk, nh).transpose(1, 0, 3, 2)

    causal = jnp.tril(jnp.ones((chunk, chunk), dtype=bool))

    def step(carry, inp):
        c, n, m = carry
        qc, kc, vc, li, lf = inp
        b = jnp.cumsum(lf, axis=-1)
        d = jnp.where(causal, b[..., :, None] - b[..., None, :] + li[..., None, :], -jnp.inf)
        inter = b + m[..., None]
        m_loc = jnp.maximum(inter, jnp.max(d, axis=-1))
        w_intra = jnp.exp(d - m_loc[..., None])
        w_inter = jnp.exp(inter - m_loc)
        sw = jnp.einsum('bhtd,bhsd->bhts', qc, kc) * w_intra
        num = w_inter[..., None] * jnp.einsum('bhtd,bhde->bhte', qc, c) + jnp.einsum('bhts,bhse->bhte', sw, vc)
        qn = w_inter * jnp.einsum('bhtd,bhd->bht', qc, n) + jnp.sum(sw, axis=-1)
        den = jnp.maximum(jnp.abs(qn), jnp.exp(-m_loc))
        hc = num / den[..., None]
        b_last = b[..., -1]
        g = b_last[..., None] - b + li
        m_new = jnp.maximum(b_last + m, jnp.max(g, axis=-1))
        a = jnp.exp(b_last + m - m_new)
        w = jnp.exp(g - m_new[..., None])
        c_new = a[..., None, None] * c + jnp.einsum('bhs,bhsd,bhse->bhde', w, kc, vc)
        n_new = a[..., None] * n + jnp.einsum('bhs,bhsd->bhd', w, kc)
        return (c_new, n_new, m_new), hc

    xs = (to_chunks(q), to_chunks(k), to_chunks(v), gate_chunks(logi), gate_chunks(logf))
    (c, n, m), hs = lax.scan(step, (c0.astype(F32), n0.astype(F32), m0.astype(F32)), xs)
    h = hs.transpose(1, 0, 3, 2, 4).reshape(bt, t, nh, dv)
    return h, c, n, m


def mlstm_mixer(h, w_in, b_gate, g_head, w_out, c0, n0, m0, chunk):
    bt, t, _ = h.shape
    proj = h @ w_in
    q, k, v, o, gates = jnp.split(proj, [A_HK, 2 * A_HK, 2 * A_HK + A_HV, 2 * A_HK + 2 * A_HV], axis=-1)
    q = q.reshape(bt, t, A_HEADS, A_DK)
    k = k.reshape(bt, t, A_HEADS, A_DK) * (A_DK ** -0.5)
    v = v.reshape(bt, t, A_HEADS, A_DV)
    gates = gates.astype(F32) + b_gate.astype(F32)
    logi = gates[..., :A_HEADS]
    logf = jax.nn.log_sigmoid(gates[..., A_HEADS:])
    hc, c, n, m = mlstm_cell(q, k, v, logi, logf, c0, n0, m0, chunk)
    hn = hc * lax.rsqrt(jnp.mean(hc * hc, axis=-1, keepdims=True) + EPS)
    hn = hn.reshape(bt, t, A_HV) * g_head.astype(F32)
    out = (jax.nn.sigmoid(o.astype(F32)) * hn).astype(h.dtype) @ w_out
    return out, c, n, m


def moba_attention(q, k, v, q_pos, q_block):
    bt, tq, nh, hd = q.shape
    tk = k.shape[1]
    nb = -(-tk // MOBA_BLOCK)
    pad = nb * MOBA_BLOCK - tk
    kb = jnp.pad(k, ((0, 0), (0, pad), (0, 0), (0, 0))).reshape(bt, nb, MOBA_BLOCK, nh, hd)
    vb = jnp.pad(v, ((0, 0), (0, pad), (0, 0), (0, 0))).reshape(bt, nb, MOBA_BLOCK, nh, hd)
    k_mean = jnp.mean(kb.astype(F32), axis=2)
    n_sel = min(MOBA_TOPK, nb)
    slopes = alibi_slopes()
    scale = hd ** -0.5
    b_idx = jnp.arange(bt)[:, None, None, None]
    h_idx = jnp.arange(nh)[None, None, :, None]
    blk_off = jnp.arange(MOBA_BLOCK, dtype=jnp.int32)
    nq = tq // q_block
    qs = q.reshape(bt, nq, q_block, nh, hd).transpose(1, 0, 2, 3, 4)
    ps = q_pos.reshape(nq, q_block)

    def attend(args):
        qc, pc = args
        own = pc // MOBA_BLOCK
        qf = qc.astype(F32)
        gate = jnp.einsum('bqhd,bnhd->bqhn', qf, k_mean)
        past = jnp.arange(nb)[None, :] < own[:, None]
        gate = jnp.where(past[None, :, None, :], gate, -jnp.inf)
        top_val, top_idx = lax.top_k(gate, n_sel)
        own_b = jnp.broadcast_to(own[None, :, None, None], (bt, q_block, nh, 1)).astype(top_idx.dtype)
        idx = jnp.concatenate([top_idx, own_b], axis=-1)
        ok = jnp.concatenate([jnp.isfinite(top_val), jnp.ones((bt, q_block, nh, 1), dtype=bool)], axis=-1)
        kg = kb[b_idx, idx, :, h_idx]
        vg = vb[b_idx, idx, :, h_idx]
        key_pos = idx[..., None] * MOBA_BLOCK + blk_off
        dist = pc[None, :, None, None, None] - key_pos
        s = jnp.einsum('bqhd,bqhnkd->bqhnk', qf, kg.astype(F32)) * scale
        s = s - slopes[None, None, :, None, None] * dist.astype(F32)
        s = jnp.where(ok[..., None] & (dist >= 0), s, -jnp.inf)
        p = jax.nn.softmax(s.reshape(bt, q_block, nh, -1), axis=-1).reshape(s.shape)
        out = jnp.einsum('bqhnk,bqhnkd->bqhd', p, vg.astype(F32))
        return out.astype(q.dtype)

    out = lax.map(attend, (qs, ps))
    return out.transpose(1, 0, 2, 3, 4).reshape(bt, tq, nh, hd)


def trunk(x, c_in, n_in, m_in, k_past, v_past, q_pos, chunk, q_block,
          g_norm, w_ffn_in, w_ffn_out, w_in_a, b_gate_a, g_head_a, w_out_a,
          g_kv, w_kv, w_q_b, w_out_b, g_final):
    bt, t, _ = x.shape
    cs, ns, ms = [], [], []
    k_new = v_new = k_all = v_all = None
    for l in range(DEPTH):
        x = x + 0.5 * swiglu(rmsnorm(x, g_norm[l, 0]), w_ffn_in[l, 0], w_ffn_out[l, 0])
        h = rmsnorm(x, g_norm[l, 1])
        if l < N_A_LAYERS:
            out, c, n, m = mlstm_mixer(h, w_in_a[l], b_gate_a[l], g_head_a[l], w_out_a[l],
                                       c_in[l], n_in[l], m_in[l], chunk)
            cs.append(c); ns.append(n); ms.append(m)
        else:
            j = l - N_A_LAYERS
            q = (h @ w_q_b[j]).reshape(bt, t, B_HEADS, B_HD)
            att = moba_attention(q, k_all, v_all, q_pos, q_block)
            out = att.reshape(bt, t, B_HW) @ w_out_b[j]
        x = x + out
        x = x + 0.5 * swiglu(rmsnorm(x, g_norm[l, 2]), w_ffn_in[l, 1], w_ffn_out[l, 1])
        if l == N_A_LAYERS - 1:
            kv = rmsnorm(x, g_kv) @ w_kv
            k_new, v_new = jnp.split(kv, 2, axis=-1)
            k_new = k_new.reshape(bt, t, B_HEADS, B_HD)
            v_new = v_new.reshape(bt, t, B_HEADS, B_HD)
            if k_past is None:
                k_all, v_all = k_new, v_new
            else:
                k_all = jnp.concatenate([k_past.astype(k_new.dtype), k_new], axis=1)
                v_all = jnp.concatenate([v_past.astype(v_new.dtype), v_new], axis=1)
    y = rmsnorm(x, g_final)
    return y, jnp.stack(cs), jnp.stack(ns), jnp.stack(ms), k_new, v_new


def setup_inputs(seed: int = 0) -> dict:
    key = jax.random.key(seed)
    ks = jax.random.split(key, 24)
    n_pages = PAST_LEN // PAGE_SIZE
    n_pool = (DEC_BATCH * n_pages * 5) // 4
    nrm = jax.random.normal
    x_prompt = nrm(ks[0], (BATCH, SEQ, D_MODEL), F32)
    x_sample = nrm(ks[1], (DEC_BATCH, DEC_SEQ, D_MODEL), F32)
    state_c = 0.1 * nrm(ks[2], (N_A_LAYERS, DEC_BATCH, A_HEADS, A_DK, A_DV), F32)
    state_n = 0.1 * nrm(ks[3], (N_A_LAYERS, DEC_BATCH, A_HEADS, A_DK), F32)
    state_m = 0.5 * nrm(ks[4], (N_A_LAYERS, DEC_BATCH, A_HEADS), F32)
    cache_k = nrm(ks[5], (n_pool, PAGE_SIZE, B_HEADS, B_HD), F32)
    cache_v = nrm(ks[6], (n_pool, PAGE_SIZE, B_HEADS, B_HD), F32)
    page_table = jax.random.permutation(ks[7], n_pool)[:DEC_BATCH * n_pages].reshape(DEC_BATCH, n_pages).astype(jnp.int32)
    g_norm = 1.0 + 0.02 * nrm(ks[8], (DEPTH, 3, D_MODEL), F32)
    w_ffn_in = nrm(ks[9], (DEPTH, 2, D_MODEL, 2 * D_FF), F32) * D_MODEL ** -0.5
    w_ffn_out = nrm(ks[10], (DEPTH, 2, D_FF, D_MODEL), F32) * D_FF ** -0.5
    w_in_a = nrm(ks[11], (N_A_LAYERS, D_MODEL, A_PROJ), F32) * D_MODEL ** -0.5
    b_gate_a = jnp.concatenate([0.1 * nrm(ks[12], (N_A_LAYERS, A_HEADS), F32),
                                3.0 + 0.1 * nrm(ks[13], (N_A_LAYERS, A_HEADS), F32)], axis=-1)
    g_head_a = 1.0 + 0.02 * nrm(ks[14], (N_A_LAYERS, A_HV), F32)
    w_out_a = nrm(ks[15], (N_A_LAYERS, A_HV, D_MODEL), F32) * A_HV ** -0.5
    g_kv = 1.0 + 0.02 * nrm(ks[16], (D_MODEL,), F32)
    w_kv = nrm(ks[17], (D_MODEL, 2 * B_HW), F32) * D_MODEL ** -0.5
    w_q_b = nrm(ks[18], (N_B_LAYERS, D_MODEL, B_HW), F32) * D_MODEL ** -0.5
    w_out_b = nrm(ks[19], (N_B_LAYERS, B_HW, D_MODEL), F32) * B_HW ** -0.5
    g_final = 1.0 + 0.02 * nrm(ks[20], (D_MODEL,), F32)
    return {"x_prompt": x_prompt, "x_sample": x_sample, "state_c": state_c, "state_n": state_n,
            "state_m": state_m, "cache_k": cache_k, "cache_v": cache_v, "page_table": page_table,
            "g_norm": g_norm, "w_ffn_in": w_ffn_in, "w_ffn_out": w_ffn_out, "w_in_a": w_in_a,
            "b_gate_a": b_gate_a, "g_head_a": g_head_a, "w_out_a": w_out_a, "g_kv": g_kv,
            "w_kv": w_kv, "w_q_b": w_q_b, "w_out_b": w_out_b, "g_final": g_final}


def reference(x_prompt, x_sample, state_c, state_n, state_m, cache_k, cache_v, page_table,
              g_norm, w_ffn_in, w_ffn_out, w_in_a, b_gate_a, g_head_a, w_out_a,
              g_kv, w_kv, w_q_b, w_out_b, g_final):
    weights = (g_norm, w_ffn_in, w_ffn_out, w_in_a, b_gate_a, g_head_a, w_out_a,
               g_kv, w_kv, w_q_b, w_out_b, g_final)
    bp, tp, _ = x_prompt.shape
    bs, ts, _ = x_sample.shape
    c0 = jnp.zeros((N_A_LAYERS, bp, A_HEADS, A_DK, A_DV), F32)
    n0 = jnp.zeros((N_A_LAYERS, bp, A_HEADS, A_DK), F32)
    m0 = jnp.zeros((N_A_LAYERS, bp, A_HEADS), F32)
    pos_p = jnp.arange(tp, dtype=jnp.int32)
    y_prompt, c_prompt, n_prompt, m_prompt, k_prompt, v_prompt = trunk(
        x_prompt, c0, n0, m0, None, None, pos_p, min(MLSTM_CHUNK, tp), PROMPT_Q_BLOCK, *weights)
    past_len = page_table.shape[1] * cache_k.shape[1]
    k_past = cache_k[page_table].reshape(bs, past_len, B_HEADS, B_HD)
    v_past = cache_v[page_table].reshape(bs, past_len, B_HEADS, B_HD)
    pos_s = past_len + jnp.arange(ts, dtype=jnp.int32)
    y_sample, c_sample, n_sample, m_sample, k_sample, v_sample = trunk(
        x_sample, state_c, state_n, state_m, k_past, v_past, pos_s, ts, 1, *weights)
    return (y_prompt, y_sample, c_prompt, n_prompt, m_prompt, k_prompt, v_prompt,
            c_sample, n_sample, m_sample, k_sample, v_sample)
```

```python
import functools

import jax
import jax.numpy as jnp
from jax import lax
from jax.experimental import pallas as pl
from jax.experimental.pallas import tpu as pltpu

F32 = jnp.float32
BF16 = jnp.bfloat16
EPS = 1e-6
MOBA_BLOCK = 256
MOBA_TOPK = 3
PROMPT_MLSTM_CHUNK = 128
LANES = 128
VMEM_LIMIT_BYTES = 56 * 1024 * 1024
NEG_INF = float("-inf")


def _params(*sem):
    return pltpu.CompilerParams(dimension_semantics=sem, vmem_limit_bytes=VMEM_LIMIT_BYTES)


def _rms(x, g):
    return x * lax.rsqrt(jnp.mean(x * x, axis=-1, keepdims=True) + EPS) * g


def _row_tile(m):
    return 512 if m % 512 == 0 else m


def _ffn_body(x_ref, gn_ref, win_ref, wout_ref, *rest, d_ff, fc, final_norm):
    if final_norm:
        gf_ref, o_ref, a_scr = rest
    else:
        o_ref, a_scr = rest
    x = x_ref[...]
    h = _rms(x, gn_ref[...]).astype(BF16)
    for c in range(d_ff // fc):
        g = jnp.dot(h, win_ref[:, c * fc:(c + 1) * fc], preferred_element_type=F32)
        u = jnp.dot(h, win_ref[:, d_ff + c * fc:d_ff + (c + 1) * fc], preferred_element_type=F32)
        a_scr[:, c * fc:(c + 1) * fc] = (g * jax.nn.sigmoid(g) * u).astype(BF16)
    y = x + 0.5 * jnp.dot(a_scr[...], wout_ref[...], preferred_element_type=F32)
    if final_norm:
        y = _rms(y, gf_ref[...])
    o_ref[...] = y


def _ffn(x, gn, w_in, w_out, g_final=None):
    m, d = x.shape
    d_ff = w_out.shape[0]
    tm = _row_tile(m)
    fc = 256 if d_ff % 256 == 0 else d_ff
    final_norm = g_final is not None
    in_specs = [pl.BlockSpec((tm, d), lambda i: (i, 0)),
                pl.BlockSpec((1, d), lambda i: (0, 0)),
                pl.BlockSpec(w_in.shape, lambda i: (0, 0)),
                pl.BlockSpec(w_out.shape, lambda i: (0, 0))]
    args = [x, gn.reshape(1, d), w_in, w_out]
    if final_norm:
        in_specs.append(pl.BlockSpec((1, d), lambda i: (0, 0)))
        args.append(g_final.reshape(1, d))
    return pl.pallas_call(
        functools.partial(_ffn_body, d_ff=d_ff, fc=fc, final_norm=final_norm),
        grid=(m // tm,),
        in_specs=in_specs,
        out_specs=pl.BlockSpec((tm, d), lambda i: (i, 0)),
        out_shape=jax.ShapeDtypeStruct((m, d), F32),
        scratch_shapes=[pltpu.VMEM((tm, d_ff), BF16)],
        compiler_params=_params("parallel"),
        name="ffn",
    )(*args)


def _norm_proj_body(x_ref, g_ref, w_ref, *o_refs, splits):
    h = _rms(x_ref[...], g_ref[...]).astype(BF16)
    off = 0
    for o_ref, n in zip(o_refs, splits):
        o_ref[...] = jnp.dot(h, w_ref[:, off:off + n], preferred_element_type=F32)
        off += n


def _norm_proj(x, g, w, splits):
    m, d = x.shape
    tm = _row_tile(m)
    return pl.pallas_call(
        functools.partial(_norm_proj_body, splits=splits),
        grid=(m // tm,),
        in_specs=[pl.BlockSpec((tm, d), lambda i: (i, 0)),
                  pl.BlockSpec((1, d), lambda i: (0, 0)),
                  pl.BlockSpec(w.shape, lambda i: (0, 0))],
        out_specs=[pl.BlockSpec((tm, n), lambda i: (i, 0)) for n in splits],
        out_shape=[jax.ShapeDtypeStruct((m, n), F32) for n in splits],
        compiler_params=_params("parallel"),
        name="norm_proj",
    )(x, g.reshape(1, d), w)


def _mlstm_proj_body(x_ref, g_ref, w_ref, bg_ref, q_ref, k_ref, v_ref, o_ref, gt_ref,
                     *, hk, hv, nh, k_scale):
    h = _rms(x_ref[...], g_ref[...]).astype(BF16)
    q_ref[...] = jnp.dot(h, w_ref[:, 0:hk], preferred_element_type=F32)
    k_ref[...] = jnp.dot(h, w_ref[:, hk:2 * hk], preferred_element_type=F32) * k_scale
    v_ref[...] = jnp.dot(h, w_ref[:, 2 * hk:2 * hk + hv], preferred_element_type=F32)
    o_ref[...] = jnp.dot(h, w_ref[:, 2 * hk + hv:2 * hk + 2 * hv], preferred_element_type=F32)
    z = jnp.dot(h, w_ref[:, 2 * hk + 2 * hv:], preferred_element_type=F32) + bg_ref[...]
    log_sig = jnp.minimum(z, 0.0) - jnp.log(1.0 + jnp.exp(-jnp.abs(z)))
    lane = lax.broadcasted_iota(jnp.int32, z.shape, 1)
    gt_ref[...] = jnp.where(lane < nh, z, log_sig)


def _mlstm_proj(x, g, w_pad, bg_pad, hk, hv, nh):
    m, d = x.shape
    tm = _row_tile(m)
    outs = (hk, hk, hv, hv, LANES)
    return pl.pallas_call(
        functools.partial(_mlstm_proj_body, hk=hk, hv=hv, nh=nh, k_scale=float((hk // nh) ** -0.5)),
        grid=(m // tm,),
        in_specs=[pl.BlockSpec((tm, d), lambda i: (i, 0)),
                  pl.BlockSpec((1, d), lambda i: (0, 0)),
                  pl.BlockSpec(w_pad.shape, lambda i: (0, 0)),
                  pl.BlockSpec((1, LANES), lambda i: (0, 0))],
        out_specs=[pl.BlockSpec((tm, n), lambda i: (i, 0)) for n in outs],
        out_shape=[jax.ShapeDtypeStruct((m, n), F32) for n in outs],
        compiler_params=_params("parallel"),
        name="mlstm_proj",
    )(x, g.reshape(1, d), w_pad, bg_pad)


def _split3_dot(a, b, a_is_mask):
    x = b if a_is_mask else a
    hi = x.astype(BF16)
    r1 = x - hi.astype(F32)
    mid = r1.astype(BF16)
    lo = (r1 - mid.astype(F32)).astype(BF16)
    if a_is_mask:
        return sum(jnp.dot(a, p, preferred_element_type=F32) for p in (hi, mid, lo))
    return sum(jnp.dot(p, b, preferred_element_type=F32) for p in (hi, mid, lo))


def _mlstm_cell_body(q_ref, k_ref, v_ref, gc_ref, gr_ref, c0_ref, n0_ref, m0_ref,
                     h_ref, c_ref, n_ref, m_ref, caug, mscr, *, nh, dk, dv, lq, ls):
    ci = pl.program_id(1)
    nc = pl.num_programs(1)

    @pl.when(ci == 0)
    def _():
        lane = lax.broadcasted_iota(jnp.int32, (dk, dv), 1)
        for hh in range(nh):
            caug[hh, :, 0:dv] = c0_ref[0, hh]
            caug[hh, :, dv:2 * dv] = jnp.where(lane == 0, n0_ref[0, hh], 0.0)
        mscr[...] = m0_ref[0]

    def pad_rows(a):
        if ls == lq:
            return a
        return jnp.concatenate([a, jnp.zeros((ls - lq, a.shape[1]), a.dtype)], axis=0)

    gcol = pad_rows(gc_ref[...])
    grow = gr_ref[0]
    t_i = lax.broadcasted_iota(jnp.int32, (lq, ls), 0)
    s_i = lax.broadcasted_iota(jnp.int32, (lq, ls), 1)
    causal = t_i >= s_i
    tril = causal.astype(BF16)
    r_i = lax.broadcasted_iota(jnp.int32, (ls, ls), 0)
    c_i = lax.broadcasted_iota(jnp.int32, (ls, ls), 1)
    triu = (r_i <= c_i).astype(BF16)
    if ls != lq:
        s_row = lax.broadcasted_iota(jnp.int32, grow.shape, 1)
        grow_f = jnp.where(s_row < lq, grow, 0.0)
    else:
        grow_f = grow
    b_cols = _split3_dot(tril, gcol, True)
    b_rows = _split3_dot(grow_f, triu, False)
    ones_col = (lax.broadcasted_iota(jnp.int32, (ls, dv), 1) == 0).astype(F32)
    s_valid = lax.broadcasted_iota(jnp.int32, (1, ls), 1) < lq
    last = lq - 1

    for hh in range(nh):
        qc = q_ref[:, hh * dk:(hh + 1) * dk].astype(BF16)
        kc = pad_rows(k_ref[:, hh * dk:(hh + 1) * dk]).astype(BF16)
        vaug = jnp.concatenate([pad_rows(v_ref[:, hh * dv:(hh + 1) * dv]), ones_col], axis=1)
        m_prev = mscr[hh:hh + 1, :]
        b_col = b_cols[:, nh + hh:nh + hh + 1]
        b_row = b_rows[nh + hh:nh + hh + 1, :]
        li_row = jnp.where(s_valid, grow[hh:hh + 1, :], NEG_INF)
        li_col = gcol[:, hh:hh + 1]
        d = jnp.where(causal, b_col - b_row + li_row, NEG_INF)
        inter = b_col + m_prev
        m_loc = jnp.maximum(inter, jnp.max(d, axis=-1, keepdims=True))
        w_intra = jnp.exp(d - m_loc)
        w_inter = jnp.exp(inter - m_loc)
        s = lax.dot_general(qc, kc, (((1,), (1,)), ((), ())), preferred_element_type=F32)
        sw = (s * w_intra).astype(BF16)
        c_old = caug[hh]
        tot = (w_inter * jnp.dot(qc, c_old.astype(BF16), preferred_element_type=F32)
               + jnp.dot(sw, vaug.astype(BF16), preferred_element_type=F32))
        num = tot[:, 0:dv]
        qn = tot[:, dv:dv + 1]
        den = jnp.maximum(jnp.abs(qn), jnp.exp(-m_loc))
        hc = num / den
        hn = hc * lax.rsqrt(jnp.mean(hc * hc, axis=-1, keepdims=True) + EPS)
        h_ref[:, hh * dv:(hh + 1) * dv] = hn
        b_last = b_col[last:last + 1, :]
        g_row = b_last - b_row + li_row
        m_new = jnp.maximum(b_last + m_prev, jnp.max(g_row, axis=-1, keepdims=True))
        a = jnp.exp(b_last + m_prev - m_new)
        bc_full = pad_rows(b_col)
        w_col = jnp.exp(b_last - bc_full + li_col - m_new)
        if ls != lq:
            w_col = jnp.where(lax.broadcasted_iota(jnp.int32, (ls, 1), 0) < lq, w_col, 0.0)
        upd = lax.dot_general(kc, (w_col * vaug).astype(BF16), (((0,), (0,)), ((), ())),
                              preferred_element_type=F32)
        caug[hh] = a * c_old + upd
        mscr[hh:hh + 1, :] = m_new

    @pl.when(ci == nc - 1)
    def _():
        for hh in range(nh):
            c_ref[0, hh] = caug[hh, :, 0:dv]
            n_ref[0, hh] = caug[hh, :, dv:dv + 1]
        m_ref[0] = mscr[...]


def _mlstm_cell(q, k, v, gates, c0, n0, m0, bt, t, chunk):
    nh, dk, dv = c0.shape[1:]
    lq = chunk
    ls = max(chunk, LANES)
    nc = t // lq
    grow = gates[:, :2 * nh].reshape(bt, t, 2 * nh).transpose(0, 2, 1)
    if ls != lq:
        assert nc == 1
        grow = jnp.pad(grow, ((0, 0), (0, 0), (0, ls - lq)))
    row_blk = lambda b, c: (b * nc + c, 0)
    st4 = lambda b, c: (b, 0, 0, 0)
    st3 = lambda b, c: (b, 0, 0)
    h, c, n, m = pl.pallas_call(
        functools.partial(_mlstm_cell_body, nh=nh, dk=dk, dv=dv, lq=lq, ls=ls),
        grid=(bt, nc),
        in_specs=[pl.BlockSpec((lq, nh * dk), row_blk),
                  pl.BlockSpec((lq, nh * dk), row_blk),
                  pl.BlockSpec((lq, nh * dv), row_blk),
                  pl.BlockSpec((lq, LANES), row_blk),
                  pl.BlockSpec((1, 2 * nh, ls), lambda b, c: (b, 0, c)),
                  pl.BlockSpec((1, nh, dk, dv), st4),
                  pl.BlockSpec((1, nh, dk, 1), st4),
                  pl.BlockSpec((1, nh, 1), st3)],
        out_specs=[pl.BlockSpec((lq, nh * dv), row_blk),
                   pl.BlockSpec((1, nh, dk, dv), st4),
                   pl.BlockSpec((1, nh, dk, 1), st4),
                   pl.BlockSpec((1, nh, 1), st3)],
        out_shape=[jax.ShapeDtypeStruct((bt * t, nh * dv), F32),
                   jax.ShapeDtypeStruct((bt, nh, dk, dv), F32),
                   jax.ShapeDtypeStruct((bt, nh, dk, 1), F32),
                   jax.ShapeDtypeStruct((bt, nh, 1), F32)],
        scratch_shapes=[pltpu.VMEM((nh, dk, 2 * dv), F32), pltpu.VMEM((nh, 1), F32)],
        compiler_params=_params("parallel", "arbitrary"),
        name="mlstm_cell",
    )(q, k, v, gates, grow, c0, n0.reshape(bt, nh, dk, 1), m0.reshape(bt, nh, 1))
    return h, c, n.reshape(bt, nh, dk), m.reshape(bt, nh)


def _out_proj_body(*refs, gated):
    if gated:
        x_ref, a_ref, o_ref, gh_ref, w_ref, y_ref = refs
        a = jax.nn.sigmoid(o_ref[...]) * (a_ref[...] * gh_ref[...])
    else:
        x_ref, a_ref, w_ref, y_ref = refs
        a = a_ref[...]
    y_ref[...] = x_ref[...] + jnp.dot(a.astype(BF16), w_ref[...], preferred_element_type=F32)


def _out_proj(x, a, w, o=None, g_head=None):
    m, d = x.shape
    ka = a.shape[1]
    tm = _row_tile(m)
    gated = o is not None
    row = lambda i: (i, 0)
    fixed = lambda i: (0, 0)
    if gated:
        in_specs = [pl.BlockSpec((tm, d), row), pl.BlockSpec((tm, ka), row), pl.BlockSpec((tm, ka), row),
                    pl.BlockSpec((1, ka), fixed), pl.BlockSpec(w.shape, fixed)]
        args = (x, a, o, g_head.reshape(1, ka), w)
    else:
        in_specs = [pl.BlockSpec((tm, d), row), pl.BlockSpec((tm, ka), row), pl.BlockSpec(w.shape, fixed)]
        args = (x, a, w)
    return pl.pallas_call(
        functools.partial(_out_proj_body, gated=gated),
        grid=(m // tm,),
        in_specs=in_specs,
        out_specs=pl.BlockSpec((tm, d), row),
        out_shape=jax.ShapeDtypeStruct((m, d), F32),
        compiler_params=_params("parallel"),
        name="out_proj",
    )(*args)


def _rank_rows(g, valid, nb):
    gm = jnp.where(valid, g, NEG_INF)
    jidx = lax.broadcasted_iota(jnp.int32, g.shape, 0)
    cnt = jnp.zeros(g.shape, F32)
    for jp in range(nb):
        row = gm[jp:jp + 1, :]
        beats = (row > gm) | ((row == gm) & (jp < jidx))
        cnt = cnt + jnp.where(beats, 1.0, 0.0)
    return valid & (cnt < float(MOBA_TOPK)) & (gm > NEG_INF)


def _moba_prompt_body(slope_ref, q_ref, k_ref, v_ref, o_ref, kb, vt, kmean, sel_scr, *, nb, hd):
    hh = pl.program_id(1)
    qb = pl.program_id(2)
    blk = MOBA_BLOCK
    scale = float(hd ** -0.5)

    @pl.when(qb == 0)
    def _():
        kf = k_ref[...]
        kb[...] = kf.astype(BF16)
        kmean[...] = jnp.sum(kf.reshape(nb, blk, hd), axis=1) * (1.0 / blk)
        vt[...] = v_ref[...].T.astype(BF16)

    slope = slope_ref[hh]
    qf = q_ref[...]
    q16 = qf.astype(BF16)
    gate_t = lax.dot_general(kmean[...], qf, (((1,), (1,)), ((), ())), preferred_element_type=F32)
    jidx = lax.broadcasted_iota(jnp.int32, (nb, blk), 0)
    sel = _rank_rows(gate_t, jidx < qb, nb)
    sel_scr[...] = jnp.where(sel, 1.0, 0.0)

    k_i = lax.broadcasted_iota(jnp.int32, (blk, blk), 0)
    q_i = lax.broadcasted_iota(jnp.int32, (blk, blk), 1)
    rel = (q_i - k_i).astype(F32)

    def scores(j):
        kj = kb[pl.ds(pl.multiple_of(j * blk, blk), blk), :]
        st = lax.dot_general(kj, q16, (((1,), (1,)), ((), ())), preferred_element_type=F32)
        dist = rel + ((qb - j) * blk).astype(F32)
        return st * scale - slope * dist

    def pv(j, p):
        vj = vt[:, pl.ds(pl.multiple_of(j * blk, blk), blk)]
        return jnp.dot(vj, p.astype(BF16), preferred_element_type=F32)

    st = jnp.where(q_i >= k_i, scores(qb), NEG_INF)
    m0 = jnp.max(st, axis=0, keepdims=True)
    p0 = jnp.exp(st - m0)
    l0 = jnp.sum(p0, axis=0, keepdims=True)
    acc0 = pv(qb, p0)

    def body(j, carry):
        m, l, acc = carry
        st = jnp.where(sel_scr[pl.ds(j, 1), :] > 0.0, scores(j), NEG_INF)
        m_new = jnp.maximum(m, jnp.max(st, axis=0, keepdims=True))
        alpha = jnp.exp(m - m_new)
        p = jnp.exp(st - m_new)
        return m_new, alpha * l + jnp.sum(p, axis=0, keepdims=True), alpha * acc + pv(j, p)

    m, l, acc = lax.fori_loop(0, qb, body, (m0, l0, acc0))
    o_ref[...] = (acc / l).T


def _moba_prompt(q, k, v, slopes, bt, t, nh, hd):
    nb = t // MOBA_BLOCK
    return pl.pallas_call(
        functools.partial(_moba_prompt_body, nb=nb, hd=hd),
        grid_spec=pltpu.PrefetchScalarGridSpec(
            num_scalar_prefetch=1,
            grid=(bt, nh, nb),
            in_specs=[pl.BlockSpec((MOBA_BLOCK, hd), lambda b, h, i, s: (b * nb + i, h)),
                      pl.BlockSpec((t, hd), lambda b, h, i, s: (b, h)),
                      pl.BlockSpec((t, hd), lambda b, h, i, s: (b, h))],
            out_specs=pl.BlockSpec((MOBA_BLOCK, hd), lambda b, h, i, s: (b * nb + i, h)),
            scratch_shapes=[pltpu.VMEM((t, hd), BF16), pltpu.VMEM((hd, t), BF16),
                            pltpu.VMEM((nb, hd), F32), pltpu.VMEM((nb, MOBA_BLOCK), F32)]),
        out_shape=jax.ShapeDtypeStruct((bt * t, nh * hd), F32),
        compiler_params=_params("parallel", "parallel", "arbitrary"),
        name="moba_prompt",
    )(slopes, q, k, v)


def _moba_sample_body(pt_ref, q_ref, kn_ref, vn_ref, k0_ref, k1_ref, v0_ref, v1_ref, o_ref,
                      qbd, qbf, m_scr, l_scr, g_scr, o_scr, *, nh, hd, ts, nb, past_len):
    j = pl.program_id(1)
    blk = MOBA_BLOCK
    rows = nh * ts
    dm = nh * hd
    scale = float(hd ** -0.5)
    r_col = lax.broadcasted_iota(jnp.int32, (rows, 1), 0)
    slope = jnp.exp2(-8.0 * ((r_col // ts) + 1).astype(F32) / nh)
    q_pos = past_len + (r_col % ts)

    @pl.when(j == 0)
    def _():
        qt = jnp.concatenate([q_ref[...]] * nh, axis=0)
        r_i = lax.broadcasted_iota(jnp.int32, (rows, dm), 0)
        c_i = lax.broadcasted_iota(jnp.int32, (rows, dm), 1)
        qm = jnp.where((r_i // ts) == (c_i // hd), qt, 0.0)
        qbf[...] = qm
        qbd[...] = qm.astype(BF16)

    def head_diag(full):
        return jnp.concatenate([full[h * ts:(h + 1) * ts, h * hd:(h + 1) * hd] for h in range(nh)], axis=0)

    kf = jnp.concatenate([k0_ref[...], k1_ref[...]], axis=0)
    vf = jnp.concatenate([v0_ref[...], v1_ref[...]], axis=0)
    kmean = jnp.sum(kf, axis=0, keepdims=True) * (1.0 / blk)
    gate = jnp.sum(qbf[...] * kmean, axis=-1, keepdims=True)
    s = lax.dot_general(qbd[...], kf.astype(BF16), (((1,), (1,)), ((), ())), preferred_element_type=F32)
    k_pos = j * blk + lax.broadcasted_iota(jnp.int32, (1, blk), 1)
    s = s * scale - slope * (q_pos - k_pos).astype(F32)
    mj = jnp.max(s, axis=-1, keepdims=True)
    p = jnp.exp(s - mj)
    lj = jnp.sum(p, axis=-1, keepdims=True)
    oj = head_diag(jnp.dot(p.astype(BF16), vf.astype(BF16), preferred_element_type=F32))
    m_scr[j] = jnp.broadcast_to(mj, (rows, LANES))
    l_scr[j] = jnp.broadcast_to(lj, (rows, LANES))
    g_scr[j] = jnp.broadcast_to(gate, (rows, LANES))
    o_scr[j] = oj

    @pl.when(j == nb - 1)
    def _():
        pad = jnp.zeros((LANES - ts, dm), F32)
        kn = jnp.concatenate([kn_ref[...], pad], axis=0).astype(BF16)
        vn = jnp.concatenate([vn_ref[...], pad], axis=0).astype(BF16)
        so = lax.dot_general(qbd[...], kn, (((1,), (1,)), ((), ())), preferred_element_type=F32)
        kk = lax.broadcasted_iota(jnp.int32, (rows, LANES), 1)
        qi = lax.broadcasted_iota(jnp.int32, (rows, LANES), 0) % ts
        so = so * scale - slope * (qi - kk).astype(F32)
        so = jnp.where(kk <= qi, so, NEG_INF)
        m_own = jnp.max(so, axis=-1, keepdims=True)
        p_own = jnp.exp(so - m_own)
        l_own = jnp.sum(p_own, axis=-1, keepdims=True)
        o_own = head_diag(jnp.dot(p_own.astype(BF16), vn, preferred_element_type=F32))
        gs = [g_scr[b] for b in range(nb)]
        sels = []
        for b in range(nb):
            cnt = jnp.zeros((rows, LANES), F32)
            for bp in range(nb):
                if bp == b:
                    continue
                beats = (gs[bp] > gs[b]) | ((gs[bp] == gs[b]) & (bp < b))
                cnt = cnt + jnp.where(beats, 1.0, 0.0)
            sels.append((cnt < float(MOBA_TOPK)) & (gs[b] > NEG_INF))
        m_all = jnp.broadcast_to(m_own, (rows, LANES))
        for b in range(nb):
            m_all = jnp.where(sels[b], jnp.maximum(m_all, m_scr[b]), m_all)
        w_own = jnp.exp(m_own - m_all)
        l_all = w_own * l_own
        o_all = w_own * o_own
        for b in range(nb):
            w = jnp.where(sels[b], jnp.exp(m_scr[b] - m_all), 0.0)
            l_all = l_all + w * l_scr[b]
            o_all = o_all + w * o_scr[b]
        out = o_all / l_all
        for h in range(nh):
            o_ref[:, h * hd:(h + 1) * hd] = out[h * ts:(h + 1) * ts, :]


def _moba_sample(q, k_new, v_new, cache_k, cache_v, page_table, bs, ts, nh, hd):
    n_pool, page = cache_k.shape[:2]
    n_pages = page_table.shape[1]
    past_len = n_pages * page
    assert past_len % MOBA_BLOCK == 0 and MOBA_BLOCK % page == 0 and MOBA_BLOCK // page == 2
    assert ts <= MOBA_BLOCK and ts % 8 == 0 and ts <= LANES and hd == LANES
    nb = past_len // MOBA_BLOCK
    dm = nh * hd
    rows = nh * ts
    ck = cache_k.reshape(n_pool, page, dm)
    cv = cache_v.reshape(n_pool, page, dm)
    pt = page_table.reshape(-1)
    tok = lambda b, j, pt: (b, 0)
    pg0 = lambda b, j, pt: (pt[b * n_pages + 2 * j], 0, 0)
    pg1 = lambda b, j, pt: (pt[b * n_pages + 2 * j + 1], 0, 0)
    return pl.pallas_call(
        functools.partial(_moba_sample_body, nh=nh, hd=hd, ts=ts, nb=nb, past_len=past_len),
        grid_spec=pltpu.PrefetchScalarGridSpec(
            num_scalar_prefetch=1,
            grid=(bs, nb),
            in_specs=[pl.BlockSpec((ts, dm), tok), pl.BlockSpec((ts, dm), tok), pl.BlockSpec((ts, dm), tok),
                      pl.BlockSpec((None, page, dm), pg0), pl.BlockSpec((None, page, dm), pg1),
                      pl.BlockSpec((None, page, dm), pg0), pl.BlockSpec((None, page, dm), pg1)],
            out_specs=pl.BlockSpec((ts, dm), tok),
            scratch_shapes=[pltpu.VMEM((rows, dm), BF16), pltpu.VMEM((rows, dm), F32),
                            pltpu.VMEM((nb, rows, LANES), F32), pltpu.VMEM((nb, rows, LANES), F32),
                            pltpu.VMEM((nb, rows, LANES), F32), pltpu.VMEM((nb, rows, hd), F32)]),
        out_shape=jax.ShapeDtypeStruct((bs * ts, dm), F32),
        compiler_params=_params("parallel", "arbitrary"),
        name="moba_sample",
    )(pt, q, k_new, v_new, ck, ck, cv, cv)


def _trunk(x, c_in, n_in, m_in, paged, chunk, w):
    bt, t, d = x.shape
    depth = w["g_norm"].shape[0]
    n_a = w["w_in_a"].shape[0]
    nh_a, dk, dv = c_in.shape[2:]
    hk, hv = nh_a * dk, nh_a * dv
    nh_b = w["slopes"].shape[0]
    hd = w["w_q_b"].shape[2] // nh_b
    x = x.reshape(bt * t, d)
    cs, ns, ms = [], [], []
    k_new = v_new = None
    for l in range(depth):
        x = _ffn(x, w["g_norm"][l, 0], w["w_ffn_in"][l, 0], w["w_ffn_out"][l, 0])
        if l < n_a:
            q, k, v, o, gates = _mlstm_proj(x, w["g_norm"][l, 1], w["w_in_a"][l], w["b_gate_a"][l], hk, hv, nh_a)
            hn, c, n, m = _mlstm_cell(q, k, v, gates, c_in[l], n_in[l], m_in[l], bt, t, chunk)
            cs.append(c); ns.append(n); ms.append(m)
            x = _out_proj(x, hn, w["w_out_a"][l], o=o, g_head=w["g_head_a"][l])
        else:
            jb = l - n_a
            (q,) = _norm_proj(x, w["g_norm"][l, 1], w["w_q_b"][jb], (nh_b * hd,))
            if paged is None:
                att = _moba_prompt(q, k_new, v_new, w["slopes"], bt, t, nh_b, hd)
            else:
                att = _moba_sample(q, k_new, v_new, *paged, bt, t, nh_b, hd)
            x = _out_proj(x, att, w["w_out_b"][jb])
        last = l == depth - 1
        x = _ffn(x, w["g_norm"][l, 2], w["w_ffn_in"][l, 1], w["w_ffn_out"][l, 1],
                 g_final=w["g_final"] if last else None)
        if l == n_a - 1:
            k_new, v_new = _norm_proj(x, w["g_kv"], w["w_kv"], (nh_b * hd, nh_b * hd))
    y = x.reshape(bt, t, d)
    return (y, jnp.stack(cs), jnp.stack(ns), jnp.stack(ms),
            k_new.reshape(bt, t, nh_b, hd), v_new.reshape(bt, t, nh_b, hd))


def kernel(x_prompt, x_sample, state_c, state_n, state_m, cache_k, cache_v, page_table,
           g_norm, w_ffn_in, w_ffn_out, w_in_a, b_gate_a, g_head_a, w_out_a,
           g_kv, w_kv, w_q_b, w_out_b, g_final):
    n_a, _, nh_a, dk, dv = state_c.shape
    nh_b = cache_k.shape[2]
    a_proj = w_in_a.shape[2]
    col_pad = -a_proj % LANES
    w = {
        "g_norm": g_norm,
        "w_ffn_in": w_ffn_in.astype(BF16),
        "w_ffn_out": w_ffn_out.astype(BF16),
        "w_in_a": jnp.pad(w_in_a, ((0, 0), (0, 0), (0, col_pad))).astype(BF16),
        "b_gate_a": jnp.pad(b_gate_a, ((0, 0), (0, LANES - b_gate_a.shape[1]))).reshape(n_a, 1, LANES),
        "g_head_a": g_head_a,
        "w_out_a": w_out_a.astype(BF16),
        "g_kv": g_kv,
        "w_kv": w_kv.astype(BF16),
        "w_q_b": w_q_b.astype(BF16),
        "w_out_b": w_out_b.astype(BF16),
        "g_final": g_final,
        "slopes": 2.0 ** (-8.0 * jnp.arange(1, nh_b + 1, dtype=F32) / nh_b),
    }
    bp, tp, _ = x_prompt.shape
    bs, ts, _ = x_sample.shape
    c0 = jnp.zeros((n_a, bp, nh_a, dk, dv), F32)
    n0 = jnp.zeros((n_a, bp, nh_a, dk), F32)
    m0 = jnp.zeros((n_a, bp, nh_a), F32)
    yp, cp, np_, mp, kp, vp = _trunk(x_prompt, c0, n0, m0, None, min(PROMPT_MLSTM_CHUNK, tp), w)
    ys, cs, ns, ms, ks, vs = _trunk(x_sample, state_c, state_n, state_m,
                                    (cache_k, cache_v, page_table), ts, w)
    return (yp, ys, cp, np_, mp, kp, vp, cs, ns, ms, ks, vs)
```
